```python
import jax, jax.numpy as jnp
from jax import lax
import numpy as np

D_MODEL = 1024
BATCH = 8
SEQ = 4096
DEPTH = 1

N_META = 16
SWA_HEADS = 8
SWA_KV_HEADS = 2
SWA_HEAD_DIM = 64
SWA_WIDTH = SWA_HEADS * SWA_HEAD_DIM
SWA_KV_WIDTH = SWA_KV_HEADS * SWA_HEAD_DIM
WINDOW = 128
GLA_HEADS = 4
GLA_WIDTH = D_MODEL - SWA_WIDTH
GLA_V_DIM = GLA_WIDTH // GLA_HEADS
GLA_K_DIM = GLA_V_DIM // 2
GLA_K_WIDTH = GLA_HEADS * GLA_K_DIM
GLA_GATE_RANK = 16
GLA_TAU = 16.0
GLA_CHUNK = 64
D_FF = ((8 * D_MODEL + 3 * 256 - 1) // (3 * 256)) * 256
LN_EPS = 1e-5
RMS_EPS = 1e-6
ALPHA = (2.0 * DEPTH) ** 0.25
BETA = (8.0 * DEPTH) ** -0.25
IN_SPLITS = (SWA_WIDTH, SWA_KV_WIDTH, SWA_KV_WIDTH,
             GLA_K_WIDTH, GLA_K_WIDTH, GLA_WIDTH,
             GLA_WIDTH, GLA_GATE_RANK)
D_IN = int(sum(IN_SPLITS))
NEG_INF = -1e30

kernel_name = "hymba_gla_swa_sink_alibi_deepnorm"


def layer_norm(x, g, b):
    xf = x.astype(jnp.float32)
    mu = jnp.mean(xf, axis=-1, keepdims=True)
    var = jnp.mean(jnp.square(xf - mu), axis=-1, keepdims=True)
    y = (xf - mu) * lax.rsqrt(var + LN_EPS)
    return (y * g.astype(jnp.float32) + b.astype(jnp.float32)).astype(x.dtype)


def rms_norm(x, g):
    xf = x.astype(jnp.float32)
    y = xf * lax.rsqrt(jnp.mean(jnp.square(xf), axis=-1, keepdims=True) + RMS_EPS)
    return (y * g.astype(jnp.float32)).astype(x.dtype)


def alibi_slopes(n_heads):
    return jnp.asarray(2.0 ** (-8.0 * (np.arange(n_heads) + 1) / n_heads), dtype=jnp.float32)


def sink_softmax(scores, sink):
    sink = jnp.broadcast_to(sink, scores.shape[:-1] + (1,)).astype(jnp.float32)
    p = jax.nn.softmax(jnp.concatenate([scores, sink], axis=-1), axis=-1)
    return p[..., :-1]


def sliding_window_gqa(q, k, v, sinks):
    B, L, Hq, dh = q.shape
    Hkv = k.shape[2]
    G = Hq // Hkv
    S = L - N_META
    nb = S // WINDOW
    scale = dh ** -0.5
    slopes = alibi_slopes(Hq).reshape(Hkv, G)
    sink = sinks.astype(jnp.float32).reshape(Hkv, G)
    q = q.reshape(B, L, Hkv, G, dh)
    qm, qr = q[:, :N_META], q[:, N_META:]
    km, kr = k[:, :N_META], k[:, N_META:]
    vm, vr = v[:, :N_META], v[:, N_META:]

    qb = qr.reshape(B, nb, WINDOW, Hkv, G, dh)
    kb = kr.reshape(B, nb, WINDOW, Hkv, dh)
    vb = vr.reshape(B, nb, WINDOW, Hkv, dh)
    pad = ((0, 0), (1, 0), (0, 0), (0, 0), (0, 0))
    kband = jnp.concatenate([jnp.pad(kb, pad)[:, :-1], kb], axis=2)
    vband = jnp.concatenate([jnp.pad(vb, pad)[:, :-1], vb], axis=2)
    s_band = jnp.einsum('bnikgd,bnjkd->bkgnij', qb, kband).astype(jnp.float32) * scale
    s_meta = jnp.einsum('bnikgd,bmkd->bkgnim', qb, km).astype(jnp.float32) * scale
    qi = jnp.arange(nb)[:, None] * WINDOW + jnp.arange(WINDOW)[None, :]
    kj = jnp.arange(nb)[:, None] * WINDOW - WINDOW + jnp.arange(2 * WINDOW)[None, :]
    dist_band = qi[:, :, None] - kj[:, None, :]
    valid = (dist_band >= 0) & (dist_band < WINDOW) & (kj[:, None, :] >= 0)
    dist_meta = (qi + N_META)[:, :, None] - jnp.arange(N_META)[None, None, :]
    sl = slopes[:, :, None, None, None]
    s_band = jnp.where(valid, s_band - sl * dist_band.astype(jnp.float32), NEG_INF)
    s_meta = s_meta - sl * dist_meta.astype(jnp.float32)
    p = sink_softmax(jnp.concatenate([s_meta, s_band], axis=-1), sink[None, :, :, None, None, None])
    p = p.astype(v.dtype)
    o_real = (jnp.einsum('bkgnim,bmkd->bnikgd', p[..., :N_META], vm)
              + jnp.einsum('bkgnij,bnjkd->bnikgd', p[..., N_META:], vband))
    o_real = o_real.reshape(B, S, Hq * dh)

    s_mm = jnp.einsum('bikgd,bjkd->bkgij', qm, km).astype(jnp.float32) * scale
    dist_mm = jnp.arange(N_META)[:, None] - jnp.arange(N_META)[None, :]
    s_mm = jnp.where(dist_mm >= 0, s_mm - slopes[:, :, None, None] * dist_mm.astype(jnp.float32), NEG_INF)
    p_mm = sink_softmax(s_mm, sink[None, :, :, None, None]).astype(v.dtype)
    o_meta = jnp.einsum('bkgij,bjkd->bikgd', p_mm, vm).reshape(B, N_META, Hq * dh)
    return jnp.concatenate([o_meta, o_real], axis=1)


def chunked_gla(q, k, v, log_g):
    B, L, H, dk = q.shape
    dv = v.shape[-1]
    C = GLA_CHUNK
    pad_front = (-L) % C
    padw = ((0, 0), (pad_front, 0), (0, 0), (0, 0))
    q = jnp.pad(q * (dk ** -0.5), padw)
    k = jnp.pad(k, padw)
    v = jnp.pad(v, padw)
    log_g = jnp.pad(log_g, padw)
    Lp = L + pad_front
    n = Lp // C

    def to_chunks(t):
        return t.reshape(B, n, C, H, t.shape[-1]).transpose(1, 0, 3, 2, 4)

    qc, kc, vc = to_chunks(q), to_chunks(k), to_chunks(v)
    bc = jnp.cumsum(to_chunks(log_g).astype(jnp.float32), axis=3)
    causal = jnp.tril(jnp.ones((C, C), dtype=bool))[..., None]

    def step(state, inp):
        qx, kx, vx, bx = inp
        o_inter = jnp.einsum('bhcd,bhde->bhce', qx * jnp.exp(bx), state)
        diff = bx[:, :, :, None, :] - bx[:, :, None, :, :]
        decay = jnp.exp(jnp.where(causal, diff, NEG_INF))
        A = jnp.einsum('bhid,bhjd,bhijd->bhij', qx.astype(jnp.float32), kx.astype(jnp.float32), decay)
        o_intra = jnp.einsum('bhij,bhje->bhie', A, vx.astype(jnp.float32))
        b_last = bx[:, :, -1:, :]
        state = (jnp.exp(b_last[:, :, 0, :])[..., None] * state
                 + jnp.einsum('bhjd,bhje->bhde', kx * jnp.exp(b_last - bx), vx.astype(jnp.float32)))
        return state, o_inter + o_intra

    s0 = jnp.zeros((B, H, dk, dv), dtype=jnp.float32)
    _, o = lax.scan(step, s0, (qc, kc, vc, bc))
    o = o.transpose(1, 0, 3, 2, 4).reshape(B, Lp, H, dv)[:, pad_front:]
    return o.astype(v.dtype)


def hybrid_mixer(h, w_in, b_in, w_gate_lr2, b_gate_lr2, sinks, gla_norm_g, w_out):
    B, L, _ = h.shape
    proj = jnp.einsum('bld,de->ble', h, w_in) + b_in
    cuts = tuple(int(c) for c in np.cumsum(IN_SPLITS)[:-1])
    q_s, k_s, v_s, q_g, k_g, v_g, r_g, g_lr = jnp.split(proj, cuts, axis=-1)
    o_s = sliding_window_gqa(q_s.reshape(B, L, SWA_HEADS, SWA_HEAD_DIM),
                             k_s.reshape(B, L, SWA_KV_HEADS, SWA_HEAD_DIM),
                             v_s.reshape(B, L, SWA_KV_HEADS, SWA_HEAD_DIM), sinks)
    gate_logit = jnp.einsum('blr,rk->blk', g_lr, w_gate_lr2) + b_gate_lr2
    log_g = jax.nn.log_sigmoid(gate_logit.astype(jnp.float32)) / GLA_TAU
    o_g = chunked_gla(q_g.reshape(B, L, GLA_HEADS, GLA_K_DIM),
                      k_g.reshape(B, L, GLA_HEADS, GLA_K_DIM),
                      v_g.reshape(B, L, GLA_HEADS, GLA_V_DIM),
                      log_g.reshape(B, L, GLA_HEADS, GLA_K_DIM))
    o_g = rms_norm(o_g, gla_norm_g).reshape(B, L, GLA_WIDTH) * jax.nn.silu(r_g)
    o = jnp.concatenate([o_s, o_g], axis=-1)
    return jnp.einsum('ble,ed->bld', o, w_out)


def swiglu(h, w_gate, w_up, w_down):
    a = jax.nn.silu(jnp.einsum('bld,df->blf', h, w_gate)) * jnp.einsum('bld,df->blf', h, w_up)
    return jnp.einsum('blf,fd->bld', a, w_down)


def setup_inputs(seed: int = 0) -> dict:
    key = jax.random.key(seed)
    ks = jax.random.split(key, 20)
    f32 = jnp.float32
    nrm = lambda k, shape, s: jax.random.normal(k, shape, dtype=f32) * s
    col_scale = np.ones((D_IN,), dtype=np.float32)
    off = np.concatenate([[0], np.cumsum(IN_SPLITS)])
    col_scale[off[2]:off[3]] = BETA
    col_scale[off[5]:off[6]] = BETA
    return {
        "x": nrm(ks[0], (BATCH, SEQ, D_MODEL), 1.0),
        "meta_tokens": nrm(ks[1], (N_META, D_MODEL), 1.0),
        "ln_in_g": 1.0 + nrm(ks[2], (D_MODEL,), 0.02),
        "ln_in_b": nrm(ks[3], (D_MODEL,), 0.02),
        "w_in": nrm(ks[4], (DEPTH, D_MODEL, D_IN), D_MODEL ** -0.5) * jnp.asarray(col_scale),
        "b_in": nrm(ks[5], (DEPTH, D_IN), 0.02),
        "w_gate_lr2": nrm(ks[6], (DEPTH, GLA_GATE_RANK, GLA_K_WIDTH), GLA_GATE_RANK ** -0.5),
        "b_gate_lr2": nrm(ks[7], (DEPTH, GLA_K_WIDTH), 0.1),
        "attn_sinks": nrm(ks[8], (DEPTH, SWA_HEADS), 0.5),
        "gla_norm_g": 1.0 + nrm(ks[9], (DEPTH, GLA_V_DIM), 0.02),
        "w_out": nrm(ks[10], (DEPTH, D_MODEL, D_MODEL), BETA * D_MODEL ** -0.5),
        "ln1_g": 1.0 + nrm(ks[11], (DEPTH, D_MODEL), 0.02),
        "ln1_b": nrm(ks[12], (DEPTH, D_MODEL), 0.02),
        "w_ffn_gate": nrm(ks[13], (DEPTH, D_MODEL, D_FF), D_MODEL ** -0.5),
        "w_ffn_up": nrm(ks[14], (DEPTH, D_MODEL, D_FF), D_MODEL ** -0.5),
        "w_ffn_down": nrm(ks[15], (DEPTH, D_FF, D_MODEL), BETA * D_FF ** -0.5),
        "ln2_g": 1.0 + nrm(ks[16], (DEPTH, D_MODEL), 0.02),
        "ln2_b": nrm(ks[17], (DEPTH, D_MODEL), 0.02),
    }


def reference(x, meta_tokens, ln_in_g, ln_in_b, w_in, b_in, w_gate_lr2, b_gate_lr2,
              attn_sinks, gla_norm_g, w_out, ln1_g, ln1_b, w_ffn_gate, w_ffn_up,
              w_ffn_down, ln2_g, ln2_b):
    B = x.shape[0]
    meta = jnp.broadcast_to(meta_tokens[None].astype(x.dtype), (B, N_META, x.shape[-1]))
    h = layer_norm(jnp.concatenate([meta, x], axis=1), ln_in_g, ln_in_b)
    for l in range(DEPTH):
        mix = hybrid_mixer(h, w_in[l], b_in[l], w_gate_lr2[l], b_gate_lr2[l],
                           attn_sinks[l], gla_norm_g[l], w_out[l])
        h = layer_norm(ALPHA * h + mix, ln1_g[l], ln1_b[l])
        ffn = swiglu(h, w_ffn_gate[l], w_ffn_up[l], w_ffn_down[l])
        h = layer_norm(ALPHA * h + ffn, ln2_g[l], ln2_b[l])
    return h[:, N_META:]
```

```python
import functools

import numpy as np
import jax
import jax.numpy as jnp
from jax import lax
from jax.experimental import pallas as pl
from jax.experimental.pallas import tpu as pltpu

F32 = jnp.float32
BF16 = jnp.bfloat16

D_MODEL = 1024
N_META = 16
SWA_HEADS = 8
SWA_KV_HEADS = 2
SWA_GROUP = SWA_HEADS // SWA_KV_HEADS
SWA_HEAD_DIM = 64
SWA_WIDTH = SWA_HEADS * SWA_HEAD_DIM
SWA_KV_WIDTH = SWA_KV_HEADS * SWA_HEAD_DIM
WINDOW = 128
GLA_HEADS = 4
GLA_WIDTH = D_MODEL - SWA_WIDTH
GLA_V_DIM = GLA_WIDTH // GLA_HEADS
GLA_K_DIM = GLA_V_DIM // 2
GLA_K_WIDTH = GLA_HEADS * GLA_K_DIM
GLA_GATE_RANK = 16
GLA_TAU = 16.0
D_FF = 2816
LN_EPS = 1e-5
RMS_EPS = 1e-6
DEPTH = 1
ALPHA = (2.0 * DEPTH) ** 0.25
NEG_INF = -1e30

C_QS = 0
C_KS = C_QS + SWA_WIDTH
C_VS = C_KS + SWA_KV_WIDTH
C_QG = C_VS + SWA_KV_WIDTH
C_KG = C_QG + GLA_K_WIDTH
C_VG = C_KG + GLA_K_WIDTH
C_RG = C_VG + GLA_WIDTH
C_LR = C_RG + GLA_WIDTH
D_IN = C_LR + GLA_GATE_RANK
LANES = 128
LR_PAD = LANES
D_IN_PAD = C_LR + LR_PAD

ROW_TILE = 512
GLA_CHUNK = 64
GLA_SUB = 16
FF_CHUNK = 256
VMEM_LIMIT = 56 * 1024 * 1024

ALIBI_SLOPES = tuple(float(2.0 ** (-8.0 * (h + 1) / SWA_HEADS)) for h in range(SWA_HEADS))


def _dot(a, b):
    return jnp.dot(a, b, preferred_element_type=F32)


def _dot_nt(a, b):
    return lax.dot_general(a, b, (((1,), (1,)), ((), ())), preferred_element_type=F32)


def _dot_tn(a, b):
    return lax.dot_general(a, b, (((0,), (0,)), ((), ())), preferred_element_type=F32)


def _layer_norm(x, g, b):
    mu = jnp.mean(x, axis=-1, keepdims=True)
    xc = x - mu
    var = jnp.mean(xc * xc, axis=-1, keepdims=True)
    return xc * lax.rsqrt(var + LN_EPS) * g + b


def _split3(x):
    x1 = x.astype(BF16)
    r1 = x - x1.astype(F32)
    x2 = r1.astype(BF16)
    x3 = (r1 - x2.astype(F32)).astype(BF16)
    return x1, x2, x3


def _const_spec(shape):
    nd = len(shape)
    return pl.BlockSpec(shape, lambda *_: (0,) * nd, pipeline_mode=pl.Buffered(1))


def _in_proj_kernel(x_ref, g_ref, b_ref, w_ref, bias_ref, wg2_ref, bg2_ref,
                    qs_ref, ks_ref, vs_ref, qg_ref, kg_ref, vg_ref, rg_ref, lg_ref):
    hb = _layer_norm(x_ref[...], g_ref[...], b_ref[...]).astype(BF16)

    def proj(lo, hi):
        return _dot(hb, w_ref[:, lo:hi]) + bias_ref[:, lo:hi]

    qs_ref[...] = (proj(C_QS, C_KS) * (SWA_HEAD_DIM ** -0.5)).astype(BF16)
    ks_ref[...] = proj(C_KS, C_VS).astype(BF16)
    vs_ref[...] = proj(C_VS, C_QG).astype(BF16)
    qg_ref[...] = proj(C_QG, C_KG) * (GLA_K_DIM ** -0.5)
    kg_ref[...] = proj(C_KG, C_VG)
    vg_ref[...] = proj(C_VG, C_RG).astype(BF16)
    rg_ref[...] = proj(C_RG, C_LR)
    lr = proj(C_LR, D_IN_PAD)
    a1, a2, _ = _split3(lr)
    w1, w2, _ = _split3(wg2_ref[...])
    logit = _dot(a1, w1) + _dot(a1, w2) + _dot(a2, w1) + bg2_ref[...]
    log_sig = jnp.minimum(logit, 0.0) - jnp.log1p(jnp.exp(-jnp.abs(logit)))
    lg_ref[...] = log_sig * (1.0 / GLA_TAU)


def _in_proj(x2d, ln_g, ln_b, w_pad, b_pad, wg2_pad, bg2, tm):
    t = x2d.shape[0]
    row = lambda n: pl.BlockSpec((tm, n), lambda i: (i, 0))
    out_shape = (
        jax.ShapeDtypeStruct((t, SWA_WIDTH), BF16),
        jax.ShapeDtypeStruct((t, SWA_KV_WIDTH), BF16),
        jax.ShapeDtypeStruct((t, SWA_KV_WIDTH), BF16),
        jax.ShapeDtypeStruct((t, GLA_K_WIDTH), F32),
        jax.ShapeDtypeStruct((t, GLA_K_WIDTH), F32),
        jax.ShapeDtypeStruct((t, GLA_WIDTH), BF16),
        jax.ShapeDtypeStruct((t, GLA_WIDTH), F32),
        jax.ShapeDtypeStruct((t, GLA_K_WIDTH), F32),
    )
    return pl.pallas_call(
        _in_proj_kernel,
        out_shape=out_shape,
        grid=(t // tm,),
        in_specs=[row(D_MODEL), _const_spec((1, D_MODEL)), _const_spec((1, D_MODEL)),
                  _const_spec((D_MODEL, D_IN_PAD)), _const_spec((1, D_IN_PAD)),
                  _const_spec((LR_PAD, GLA_K_WIDTH)), _const_spec((1, GLA_K_WIDTH))],
        out_specs=tuple(row(s.shape[1]) for s in out_shape),
        compiler_params=pltpu.CompilerParams(dimension_semantics=("arbitrary",),
                                             vmem_limit_bytes=VMEM_LIMIT),
        name="in_proj",
    )(x2d, ln_g, ln_b, w_pad, b_pad, wg2_pad, bg2)


def _swa_band_bias():
    i = np.arange(WINDOW)[:, None]
    j = np.arange(2 * WINDOW)[None, :] - WINDOW
    dist = (i - j).astype(np.float32)
    valid = (dist >= 0) & (dist < WINDOW)
    out = np.empty((SWA_KV_HEADS, SWA_GROUP * WINDOW, 2 * WINDOW), np.float32)
    for h in range(SWA_HEADS):
        kv, g = divmod(h, SWA_GROUP)
        out[kv, g * WINDOW:(g + 1) * WINDOW] = np.where(valid, -np.float32(ALIBI_SLOPES[h]) * dist, NEG_INF)
    return out


def _swa_kernel(sink_ref, q_ref, kc_ref, vc_ref, kp_ref, vp_ref, km_ref, vm_ref, bias_ref,
                o_ref, kbuf, vbuf):
    t = pl.program_id(1)
    tq = q_ref.shape[0]
    kbuf[0:WINDOW, :] = kp_ref[...]
    kbuf[WINDOW:, :] = kc_ref[...]
    vbuf[0:WINDOW, :] = vp_ref[...]
    vbuf[WINDOW:, :] = vc_ref[...]
    rows = SWA_GROUP * WINDOW
    ones_band = jnp.ones((2 * WINDOW, LANES), BF16)
    ones_meta = jnp.ones((N_META, LANES), BF16)
    first_pen = jnp.where((lax.broadcasted_iota(jnp.int32, (1, 2 * WINDOW), 1) < WINDOW) & (t == 0),
                          NEG_INF, 0.0).astype(F32)
    meta_rel = (lax.broadcasted_iota(jnp.int32, (WINDOW, N_META), 0)
                - lax.broadcasted_iota(jnp.int32, (WINDOW, N_META), 1) + N_META)
    for qb in range(tq // WINDOW):
        q = q_ref[qb * WINDOW:(qb + 1) * WINDOW, :]
        kwin = kbuf[qb * WINDOW:(qb + 2) * WINDOW, :]
        vwin = vbuf[qb * WINDOW:(qb + 2) * WINDOW, :]
        dist_meta = (meta_rel + (t * tq + qb * WINDOW)).astype(F32)
        outs = []
        for kv in range(SWA_KV_HEADS):
            ksl = slice(kv * SWA_HEAD_DIM, (kv + 1) * SWA_HEAD_DIM)
            heads = [kv * SWA_GROUP + g for g in range(SWA_GROUP)]
            qst = jnp.concatenate([q[:, h * SWA_HEAD_DIM:(h + 1) * SWA_HEAD_DIM] for h in heads], axis=0)
            s_band = _dot_nt(qst, kwin[:, ksl]) + bias_ref[kv]
            if qb == 0:
                s_band = s_band + first_pen
            s_meta = _dot_nt(qst, km_ref[:, ksl]) - jnp.concatenate(
                [ALIBI_SLOPES[h] * dist_meta for h in heads], axis=0)
            sink = jnp.concatenate([jnp.full((WINDOW, 1), sink_ref[h], F32) for h in heads], axis=0)
            m = jnp.maximum(jnp.maximum(jnp.max(s_band, axis=-1, keepdims=True),
                                        jnp.max(s_meta, axis=-1, keepdims=True)), sink)
            e_band = jnp.exp(s_band - m).astype(BF16)
            e_meta = jnp.exp(s_meta - m).astype(BF16)
            numer = _dot(e_band, vwin[:, ksl]) + _dot(e_meta, vm_ref[:, ksl])
            denom = _dot(e_band, ones_band) + _dot(e_meta, ones_meta) + jnp.exp(sink - m)
            o = numer / denom[:, :SWA_HEAD_DIM]
            outs.extend(o[g * WINDOW:(g + 1) * WINDOW] for g in range(SWA_GROUP))
        o_ref[qb * WINDOW:(qb + 1) * WINDOW, :] = jnp.concatenate(outs, axis=1).astype(o_ref.dtype)
    del rows


def _swa(qs, ks, vs, km, vm, sinks, batch, seq):
    tq = ROW_TILE
    nt = seq // tq
    wpt = tq // WINDOW
    bias = jnp.asarray(_swa_band_bias())
    cur = lambda n: pl.BlockSpec((tq, n), lambda b, t, *_: (b * nt + t, 0))
    prev = pl.BlockSpec((WINDOW, SWA_KV_WIDTH), lambda b, t, *_: (jnp.maximum((b * nt + t) * wpt - 1, 0), 0))
    grid_spec = pltpu.PrefetchScalarGridSpec(
        num_scalar_prefetch=1,
        grid=(batch, nt),
        in_specs=[cur(SWA_WIDTH), cur(SWA_KV_WIDTH), cur(SWA_KV_WIDTH), prev, prev,
                  _const_spec((N_META, SWA_KV_WIDTH)), _const_spec((N_META, SWA_KV_WIDTH)),
                  _const_spec(bias.shape)],
        out_specs=cur(SWA_WIDTH),
        scratch_shapes=[pltpu.VMEM((WINDOW + tq, SWA_KV_WIDTH), BF16),
                        pltpu.VMEM((WINDOW + tq, SWA_KV_WIDTH), BF16)],
    )
    return pl.pallas_call(
        _swa_kernel,
        out_shape=jax.ShapeDtypeStruct((batch * seq, SWA_WIDTH), BF16),
        grid_spec=grid_spec,
        compiler_params=pltpu.CompilerParams(dimension_semantics=("arbitrary", "arbitrary"),
                                             vmem_limit_bytes=VMEM_LIMIT),
        name="swa",
    )(sinks, qs, ks, vs, ks, vs, km, vm, bias)


N_SUB = GLA_CHUNK // GLA_SUB
GLA_STACK = GLA_CHUNK + GLA_SUB * (N_SUB * (N_SUB - 1) // 2)


def _gla_mask():
    m = np.zeros((GLA_CHUNK, GLA_STACK), np.float32)
    i = np.arange(GLA_CHUNK)
    for r in range(GLA_CHUNK):
        m[r, :GLA_CHUNK] = (i // GLA_SUB == r // GLA_SUB) & (i <= r)
    off = GLA_CHUNK
    for blk in range(1, N_SUB):
        m[blk * GLA_SUB:(blk + 1) * GLA_SUB, off:off + blk * GLA_SUB] = 1.0
        off += blk * GLA_SUB
    return m


def _cumsum_rows(tri, x):
    n = x.shape[1]
    y = _dot(tri, jnp.concatenate(_split3(x), axis=1))
    return y[:, :n] + y[:, n:2 * n] + y[:, 2 * n:]


def _gla_kernel(q_ref, k_ref, v_ref, r_ref, lg_ref, km_ref, vm_ref, lgm_ref, gn_ref,
                tri_ref, trim_ref, mask_ref, o_ref, st_ref):
    t = pl.program_id(1)
    hk = lambda h: slice(h * GLA_K_DIM, (h + 1) * GLA_K_DIM)
    hv = lambda h: slice(h * GLA_V_DIM, (h + 1) * GLA_V_DIM)

    @pl.when(t == 0)
    def _():
        bm = _cumsum_rows(trim_ref[...], lgm_ref[...])
        kd = (km_ref[...] * jnp.exp(bm[N_META - 1:N_META] - bm)).astype(BF16)
        vm = vm_ref[...]
        for h in range(GLA_HEADS):
            st_ref[:, hk(h)] = _dot_tn(vm[:, hv(h)], kd[:, hk(h)])

    tri = tri_ref[...]
    mask = mask_ref[...] > 0.5
    gn = gn_ref[...]

    def chunk(c, carry):
        r0 = pl.multiple_of(c * GLA_CHUNK, GLA_CHUNK)
        rows = pl.ds(r0, GLA_CHUNK)
        q = q_ref[rows, :]
        k = k_ref[rows, :]
        v = v_ref[rows, :]
        b = _cumsum_rows(tri, lg_ref[rows, :])
        beta = [b[i * GLA_SUB - 1:i * GLA_SUB] for i in range(1, N_SUB)]
        bl = jnp.concatenate([b[:GLA_SUB]] + [b[i * GLA_SUB:(i + 1) * GLA_SUB] - beta[i - 1]
                                              for i in range(1, N_SUB)], axis=0)
        q_in = (q * jnp.exp(bl)).astype(BF16)
        k_parts = [k * jnp.exp(-bl)]
        v_parts = [v]
        for i in range(1, N_SUB):
            n = i * GLA_SUB
            k_parts.append(k[:n] * jnp.exp(beta[i - 1] - b[:n]))
            v_parts.append(v[:n])
        k_st = jnp.concatenate(k_parts, axis=0).astype(BF16)
        v_st = jnp.concatenate(v_parts, axis=0)
        q_x = (q * jnp.exp(b)).astype(BF16)
        b_last = b[GLA_CHUNK - 1:GLA_CHUNK]
        k_x = (k * jnp.exp(b_last - b)).astype(BF16)
        dec = jnp.exp(b_last)
        outs = []
        for h in range(GLA_HEADS):
            st = st_ref[:, hk(h)]
            sc = _dot_nt(q_in[:, hk(h)], k_st[:, hk(h)])
            p = jnp.where(mask, sc, 0.0).astype(BF16)
            o = _dot(p, v_st[:, hv(h)]) + _dot_nt(q_x[:, hk(h)], st.astype(BF16))
            st_ref[:, hk(h)] = st * dec[:, hk(h)] + _dot_tn(v[:, hv(h)], k_x[:, hk(h)])
            y = o * lax.rsqrt(jnp.mean(o * o, axis=-1, keepdims=True) + RMS_EPS) * gn
            r = r_ref[rows, hv(h)]
            outs.append(y * (r / (1.0 + jnp.exp(-r))))
        o_ref[rows, :] = jnp.concatenate(outs, axis=1).astype(o_ref.dtype)
        return carry

    lax.fori_loop(0, q_ref.shape[0] // GLA_CHUNK, chunk, 0)


def _gla(qg, kg, vg, rg, lg, kg_m, vg_m, lg_m, gn, batch, seq):
    tg = ROW_TILE
    nt = seq // tg
    cur = lambda n: pl.BlockSpec((tg, n), lambda b, t: (b * nt + t, 0))
    tri = jnp.asarray(np.tril(np.ones((GLA_CHUNK, GLA_CHUNK), np.float32)), BF16)
    tri_m = jnp.asarray(np.tril(np.ones((N_META, N_META), np.float32)), BF16)
    mask = jnp.asarray(_gla_mask())
    return pl.pallas_call(
        _gla_kernel,
        out_shape=jax.ShapeDtypeStruct((batch * seq, GLA_WIDTH), BF16),
        grid=(batch, nt),
        in_specs=[cur(GLA_K_WIDTH), cur(GLA_K_WIDTH), cur(GLA_WIDTH), cur(GLA_WIDTH), cur(GLA_K_WIDTH),
                  _const_spec((N_META, GLA_K_WIDTH)), _const_spec((N_META, GLA_WIDTH)),
                  _const_spec((N_META, GLA_K_WIDTH)), _const_spec((1, GLA_V_DIM)),
                  _const_spec(tri.shape), _const_spec(tri_m.shape), _const_spec(mask.shape)],
        out_specs=cur(GLA_WIDTH),
        scratch_shapes=[pltpu.VMEM((GLA_V_DIM, GLA_K_WIDTH), F32)],
        compiler_params=pltpu.CompilerParams(dimension_semantics=("arbitrary", "arbitrary"),
                                             vmem_limit_bytes=VMEM_LIMIT),
        name="gla",
    )(qg, kg, vg, rg, lg, kg_m, vg_m, lg_m, gn, tri, tri_m, mask)


def _ffn_kernel(x_ref, os_ref, og_ref, lng_ref, lnb_ref, wo_ref, l1g_ref, l1b_ref,
                wg_ref, wu_ref, wd_ref, l2g_ref, l2b_ref, out_ref):
    h = _layer_norm(x_ref[...], lng_ref[...], lnb_ref[...])
    mix = _dot(os_ref[...], wo_ref[:SWA_WIDTH, :]) + _dot(og_ref[...], wo_ref[SWA_WIDTH:, :])
    h1 = _layer_norm(ALPHA * h + mix, l1g_ref[...], l1b_ref[...])
    h1b = h1.astype(BF16)
    acc = jnp.zeros(h1.shape, F32)
    for c in range(0, D_FF, FF_CHUNK):
        gate = _dot(h1b, wg_ref[:, c:c + FF_CHUNK])
        up = _dot(h1b, wu_ref[:, c:c + FF_CHUNK])
        a = (gate / (1.0 + jnp.exp(-gate)) * up).astype(BF16)
        acc = acc + _dot(a, wd_ref[c:c + FF_CHUNK, :])
    out_ref[...] = _layer_norm(ALPHA * h1 + acc, l2g_ref[...], l2b_ref[...])


def _ffn(x2d, o_s, o_g, ln_g, ln_b, w_out, l1g, l1b, w_gate, w_up, w_down, l2g, l2b):
    t = x2d.shape[0]
    tm = ROW_TILE
    row = lambda n: pl.BlockSpec((tm, n), lambda i: (i, 0))
    vec = _const_spec((1, D_MODEL))
    return pl.pallas_call(
        _ffn_kernel,
        out_shape=jax.ShapeDtypeStruct((t, D_MODEL), F32),
        grid=(t // tm,),
        in_specs=[row(D_MODEL), row(SWA_WIDTH), row(GLA_WIDTH), vec, vec,
                  _const_spec((D_MODEL, D_MODEL)), vec, vec,
                  _const_spec((D_MODEL, D_FF)), _const_spec((D_MODEL, D_FF)),
                  _const_spec((D_FF, D_MODEL)), vec, vec],
        out_specs=row(D_MODEL),
        compiler_params=pltpu.CompilerParams(dimension_semantics=("arbitrary",),
                                             vmem_limit_bytes=VMEM_LIMIT),
        name="ffn",
    )(x2d, o_s, o_g, ln_g, ln_b, w_out, l1g, l1b, w_gate, w_up, w_down, l2g, l2b)


def kernel(x, meta_tokens, ln_in_g, ln_in_b, w_in, b_in, w_gate_lr2, b_gate_lr2, attn_sinks,
           gla_norm_g, w_out, ln1_g, ln1_b, w_ffn_gate, w_ffn_up, w_ffn_down, ln2_g, ln2_b):
    batch, seq, d = x.shape
    assert d == D_MODEL and seq % ROW_TILE == 0 and w_in.shape[0] == DEPTH
    x2d = x.reshape(batch * seq, d)
    vec = lambda a: a.reshape(1, -1).astype(F32)
    ln_g, ln_b = vec(ln_in_g), vec(ln_in_b)
    pad = LR_PAD - GLA_GATE_RANK
    w_pad = jnp.pad(w_in[0], ((0, 0), (0, pad))).astype(BF16)
    b_pad = jnp.pad(b_in[0], (0, pad)).reshape(1, -1)
    wg2_pad = jnp.pad(w_gate_lr2[0], ((0, pad), (0, 0)))
    bg2 = vec(b_gate_lr2[0])

    in_proj = functools.partial(_in_proj, ln_g=ln_g, ln_b=ln_b, w_pad=w_pad, b_pad=b_pad,
                                wg2_pad=wg2_pad, bg2=bg2)
    qs, ks, vs, qg, kg, vg, rg, lg = in_proj(x2d, tm=ROW_TILE)
    _, ks_m, vs_m, _, kg_m, vg_m, _, lg_m = in_proj(meta_tokens.astype(F32), tm=N_META)

    o_s = _swa(qs, ks, vs, ks_m, vs_m, attn_sinks[0].astype(F32), batch, seq)
    o_g = _gla(qg, kg, vg, rg, lg, kg_m, vg_m, lg_m, vec(gla_norm_g[0]), batch, seq)
    out = _ffn(x2d, o_s, o_g, ln_g, ln_b, w_out[0].astype(BF16), vec(ln1_g[0]), vec(ln1_b[0]),
               w_ffn_gate[0].astype(BF16), w_ffn_up[0].astype(BF16), w_ffn_down[0].astype(BF16),
               vec(ln2_g[0]), vec(ln2_b[0]))
    return out.reshape(batch, seq, d)
```

```python
import functools

import numpy as np
import jax
import jax.numpy as jnp
from jax import lax
from jax.experimental import pallas as pl
from jax.experimental.pallas import tpu as pltpu

F32 = jnp.float32
BF16 = jnp.bfloat16

D_MODEL = 1024
N_META = 16
SWA_HEADS = 8
SWA_KV_HEADS = 2
SWA_GROUP = SWA_HEADS // SWA_KV_HEADS
SWA_HEAD_DIM = 64
SWA_WIDTH = SWA_HEADS * SWA_HEAD_DIM
SWA_KV_WIDTH = SWA_KV_HEADS * SWA_HEAD_DIM
WINDOW = 128
GLA_HEADS = 4
GLA_WIDTH = D_MODEL - SWA_WIDTH
GLA_V_DIM = GLA_WIDTH // GLA_HEADS
GLA_K_DIM = GLA_V_DIM // 2
GLA_K_WIDTH = GLA_HEADS * GLA_K_DIM
GLA_GATE_RANK = 16
GLA_TAU = 16.0
D_FF = 2816
LN_EPS = 1e-5
RMS_EPS = 1e-6
DEPTH = 1
ALPHA = (2.0 * DEPTH) ** 0.25
NEG_INF = -1e30

C_QS = 0
C_KS = C_QS + SWA_WIDTH
C_VS = C_KS + SWA_KV_WIDTH
C_QG = C_VS + SWA_KV_WIDTH
C_KG = C_QG + GLA_K_WIDTH
C_VG = C_KG + GLA_K_WIDTH
C_RG = C_VG + GLA_WIDTH
C_LR = C_RG + GLA_WIDTH
D_IN = C_LR + GLA_GATE_RANK
LANES = 128
LR_PAD = LANES
D_IN_PAD = C_LR + LR_PAD

ROW_TILE = 512
GLA_CHUNK = 64
FF_CHUNK = 256
VMEM_LIMIT = 56 * 1024 * 1024

ALIBI_SLOPES = tuple(float(2.0 ** (-8.0 * (h + 1) / SWA_HEADS)) for h in range(SWA_HEADS))


def _dot(a, b):
    return jnp.dot(a, b, preferred_element_type=F32)


def _dot_nt(a, b):
    return lax.dot_general(a, b, (((1,), (1,)), ((), ())), preferred_element_type=F32)


def _dot_tn(a, b):
    return lax.dot_general(a, b, (((0,), (0,)), ((), ())), preferred_element_type=F32)


def _layer_norm(x, g, b):
    mu = jnp.mean(x, axis=-1, keepdims=True)
    xc = x - mu
    var = jnp.mean(xc * xc, axis=-1, keepdims=True)
    return xc * lax.rsqrt(var + LN_EPS) * g + b


def _split3(x):
    x1 = x.astype(BF16)
    r1 = x - x1.astype(F32)
    x2 = r1.astype(BF16)
    x3 = (r1 - x2.astype(F32)).astype(BF16)
    return x1, x2, x3


def _const_spec(shape):
    nd = len(shape)
    return pl.BlockSpec(shape, lambda *_: (0,) * nd, pipeline_mode=pl.Buffered(1))


def _in_proj_kernel(x_ref, g_ref, b_ref, w_ref, bias_ref, wg2_ref, bg2_ref,
                    qs_ref, ks_ref, vs_ref, qg_ref, kg_ref, vg_ref, rg_ref, lg_ref):
    hb = _layer_norm(x_ref[...], g_ref[...], b_ref[...]).astype(BF16)

    def proj(lo, hi):
        return _dot(hb, w_ref[:, lo:hi]) + bias_ref[:, lo:hi]

    qs_ref[...] = (proj(C_QS, C_KS) * (SWA_HEAD_DIM ** -0.5)).astype(BF16)
    ks_ref[...] = proj(C_KS, C_VS).astype(BF16)
    vs_ref[...] = proj(C_VS, C_QG).astype(BF16)
    qg_ref[...] = proj(C_QG, C_KG) * (GLA_K_DIM ** -0.5)
    kg_ref[...] = proj(C_KG, C_VG)
    vg_ref[...] = proj(C_VG, C_RG).astype(BF16)
    rg_ref[...] = proj(C_RG, C_LR)
    lr = proj(C_LR, D_IN_PAD)
    a1, a2, _ = _split3(lr)
    w1, w2, _ = _split3(wg2_ref[...])
    logit = _dot(a1, w1) + _dot(a1, w2) + _dot(a2, w1) + bg2_ref[...]
    log_sig = jnp.minimum(logit, 0.0) - jnp.log1p(jnp.exp(-jnp.abs(logit)))
    lg_ref[...] = log_sig * (1.0 / GLA_TAU)


def _in_proj(x2d, ln_g, ln_b, w_pad, b_pad, wg2_pad, bg2, tm):
    t = x2d.shape[0]
    row = lambda n: pl.BlockSpec((tm, n), lambda i: (i, 0))
    out_shape = (
        jax.ShapeDtypeStruct((t, SWA_WIDTH), BF16),
        jax.ShapeDtypeStruct((t, SWA_KV_WIDTH), BF16),
        jax.ShapeDtypeStruct((t, SWA_KV_WIDTH), BF16),
        jax.ShapeDtypeStruct((t, GLA_K_WIDTH), F32),
        jax.ShapeDtypeStruct((t, GLA_K_WIDTH), F32),
        jax.ShapeDtypeStruct((t, GLA_WIDTH), BF16),
        jax.ShapeDtypeStruct((t, GLA_WIDTH), F32),
        jax.ShapeDtypeStruct((t, GLA_K_WIDTH), F32),
    )
    return pl.pallas_call(
        _in_proj_kernel,
        out_shape=out_shape,
        grid=(t // tm,),
        in_specs=[row(D_MODEL), _const_spec((1, D_MODEL)), _const_spec((1, D_MODEL)),
                  _const_spec((D_MODEL, D_IN_PAD)), _const_spec((1, D_IN_PAD)),
                  _const_spec((LR_PAD, GLA_K_WIDTH)), _const_spec((1, GLA_K_WIDTH))],
        out_specs=tuple(row(s.shape[1]) for s in out_shape),
        compiler_params=pltpu.CompilerParams(dimension_semantics=("arbitrary",),
                                             vmem_limit_bytes=VMEM_LIMIT),
        name="in_proj",
    )(x2d, ln_g, ln_b, w_pad, b_pad, wg2_pad, bg2)


def _swa_band_bias():
    i = np.arange(WINDOW)[:, None]
    j = np.arange(2 * WINDOW)[None, :] - WINDOW
    dist = (i - j).astype(np.float32)
    valid = (dist >= 0) & (dist < WINDOW)
    out = np.empty((SWA_KV_HEADS, SWA_GROUP * WINDOW, 2 * WINDOW), np.float32)
    for h in range(SWA_HEADS):
        kv, g = divmod(h, SWA_GROUP)
        out[kv, g * WINDOW:(g + 1) * WINDOW] = np.where(valid, -np.float32(ALIBI_SLOPES[h]) * dist, NEG_INF)
    return out


def _swa_kernel(sink_ref, q_ref, kc_ref, vc_ref, kp_ref, vp_ref, km_ref, vm_ref, bias_ref,
                o_ref, kbuf, vbuf):
    t = pl.program_id(1)
    tq = q_ref.shape[0]
    kbuf[0:WINDOW, :] = kp_ref[...]
    kbuf[WINDOW:, :] = kc_ref[...]
    vbuf[0:WINDOW, :] = vp_ref[...]
    vbuf[WINDOW:, :] = vc_ref[...]
    ones_band = jnp.ones((2 * WINDOW, LANES), BF16)
    ones_meta = jnp.ones((N_META, LANES), BF16)
    first_pen = jnp.where((lax.broadcasted_iota(jnp.int32, (1, 2 * WINDOW), 1) < WINDOW) & (t == 0),
                          NEG_INF, 0.0).astype(F32)
    meta_rel = (lax.broadcasted_iota(jnp.int32, (WINDOW, N_META), 0)
                - lax.broadcasted_iota(jnp.int32, (WINDOW, N_META), 1) + N_META)
    for qb in range(tq // WINDOW):
        q = q_ref[qb * WINDOW:(qb + 1) * WINDOW, :]
        kwin = kbuf[qb * WINDOW:(qb + 2) * WINDOW, :]
        vwin = vbuf[qb * WINDOW:(qb + 2) * WINDOW, :]
        dist_meta = (meta_rel + (t * tq + qb * WINDOW)).astype(F32)
        outs = []
        for kv in range(SWA_KV_HEADS):
            ksl = slice(kv * SWA_HEAD_DIM, (kv + 1) * SWA_HEAD_DIM)
            heads = [kv * SWA_GROUP + g for g in range(SWA_GROUP)]
            qst = jnp.concatenate([q[:, h * SWA_HEAD_DIM:(h + 1) * SWA_HEAD_DIM] for h in heads], axis=0)
            s_band = _dot_nt(qst, kwin[:, ksl]) + bias_ref[kv]
            if qb == 0:
                s_band = s_band + first_pen
            s_meta = _dot_nt(qst, km_ref[:, ksl]) - jnp.concatenate(
                [ALIBI_SLOPES[h] * dist_meta for h in heads], axis=0)
            sink = jnp.concatenate([jnp.full((WINDOW, 1), sink_ref[h], F32) for h in heads], axis=0)
            m = jnp.maximum(jnp.maximum(jnp.max(s_band, axis=-1, keepdims=True),
                                        jnp.max(s_meta, axis=-1, keepdims=True)), sink)
            e_band = jnp.exp(s_band - m).astype(BF16)
            e_meta = jnp.exp(s_meta - m).astype(BF16)
            numer = _dot(e_band, vwin[:, ksl]) + _dot(e_meta, vm_ref[:, ksl])
            denom = _dot(e_band, ones_band) + _dot(e_meta, ones_meta) + jnp.exp(sink - m)
            o = numer / denom[:, :SWA_HEAD_DIM]
            outs.extend(o[g * WINDOW:(g + 1) * WINDOW] for g in range(SWA_GROUP))
        o_ref[qb * WINDOW:(qb + 1) * WINDOW, :] = jnp.concatenate(outs, axis=1).astype(o_ref.dtype)


def _swa(qs, ks, vs, km, vm, sinks, batch, seq):
    tq = ROW_TILE
    nt = seq // tq
    wpt = tq // WINDOW
    bias = jnp.asarray(_swa_band_bias())
    cur = lambda n: pl.BlockSpec((tq, n), lambda b, t, *_: (b * nt + t, 0))
    prev = pl.BlockSpec((WINDOW, SWA_KV_WIDTH), lambda b, t, *_: (jnp.maximum((b * nt + t) * wpt - 1, 0), 0))
    grid_spec = pltpu.PrefetchScalarGridSpec(
        num_scalar_prefetch=1,
        grid=(batch, nt),
        in_specs=[cur(SWA_WIDTH), cur(SWA_KV_WIDTH), cur(SWA_KV_WIDTH), prev, prev,
                  _const_spec((N_META, SWA_KV_WIDTH)), _const_spec((N_META, SWA_KV_WIDTH)),
                  _const_spec(bias.shape)],
        out_specs=cur(SWA_WIDTH),
        scratch_shapes=[pltpu.VMEM((WINDOW + tq, SWA_KV_WIDTH), BF16),
                        pltpu.VMEM((WINDOW + tq, SWA_KV_WIDTH), BF16)],
    )
    return pl.pallas_call(
        _swa_kernel,
        out_shape=jax.ShapeDtypeStruct((batch * seq, SWA_WIDTH), BF16),
        grid_spec=grid_spec,
        compiler_params=pltpu.CompilerParams(dimension_semantics=("arbitrary", "arbitrary"),
                                             vmem_limit_bytes=VMEM_LIMIT),
        name="swa",
    )(sinks, qs, ks, vs, ks, vs, km, vm, bias)


GLA_LEVELS = (32, 16, 8)
GLA_DIAG = 8
N_PAIR = GLA_HEADS // 2
PAIR_K = 2 * GLA_K_DIM
PAIR_V = 2 * GLA_V_DIM


def _gla_consts():
    assert PAIR_K == LANES and GLA_CHUNK == GLA_K_DIM and GLA_LEVELS[-1] == GLA_DIAG
    i = np.arange(GLA_CHUNK)[:, None]
    j = np.arange(PAIR_K)[None, :] % GLA_CHUNK
    level = np.stack([i // (2 * w) == j // (2 * w) for w in GLA_LEVELS]).astype(np.float32)
    band = np.stack([j == i - m for m in range(GLA_DIAG)]).astype(np.float32)
    head_of_row = np.arange(GLA_K_WIDTH)[:, None] // GLA_K_DIM
    head_of_lane = np.arange(N_PAIR * PAIR_K)[None, :] // GLA_CHUNK
    ones = (head_of_row == head_of_lane).astype(np.float32)
    tri = np.tril(np.ones((GLA_CHUNK, GLA_CHUNK), np.float32))
    tri_m = np.tril(np.ones((N_META, N_META), np.float32))
    return (jnp.asarray(tri, BF16), jnp.asarray(tri_m, BF16), jnp.asarray(level), jnp.asarray(band),
            jnp.asarray(ones, BF16))


def _split2(x):
    hi = x.astype(BF16)
    return hi, (x - hi.astype(F32)).astype(BF16)


def _gla_kernel(q_ref, k_ref, v_ref, r_ref, lg_ref, km_ref, vm_ref, lgm_ref, gn_ref,
                tri_ref, trim_ref, level_ref, band_ref, ones_ref, o_ref, st_ref):
    t = pl.program_id(1)
    n_chunk = q_ref.shape[0] // GLA_CHUNK
    rows = lambda c: slice(c * GLA_CHUNK, (c + 1) * GLA_CHUNK)
    pk = lambda p: slice(p * PAIR_K, (p + 1) * PAIR_K)
    pv = lambda p: slice(p * PAIR_V, (p + 1) * PAIR_V)
    first_head = lax.broadcasted_iota(jnp.int32, (1, PAIR_K), 1) < GLA_K_DIM

    def head_split(kp):
        zero = jnp.zeros_like(kp)
        return jnp.where(first_head, kp, zero), jnp.where(first_head, zero, kp)

    def state_increment(vp, kp):
        k0, k1 = head_split(kp)
        return jnp.concatenate([_dot_tn(vp[:, :GLA_V_DIM], k0), _dot_tn(vp[:, GLA_V_DIM:], k1)], axis=0)

    @pl.when(t == 0)
    def _():
        hi, lo = _split2(lgm_ref[...])
        trim = trim_ref[...]
        bm = _dot(trim, hi) + _dot(trim, lo)
        kd = (km_ref[...] * jnp.exp(bm[N_META - 1:N_META] - bm)).astype(BF16)
        vm = vm_ref[...]
        for p in range(N_PAIR):
            st_ref[p] = state_increment(vm[:, pv(p)], kd[:, pk(p)])

    parts = []
    for c in range(n_chunk):
        parts.extend(_split2(lg_ref[rows(c), :]))
    y = _dot(tri_ref[...], jnp.concatenate(parts, axis=1))
    kw = GLA_K_WIDTH
    b = [y[:, 2 * c * kw:(2 * c + 1) * kw] + y[:, (2 * c + 1) * kw:(2 * c + 2) * kw] for c in range(n_chunk)]

    row_in_diag = lax.broadcasted_iota(jnp.int32, (GLA_CHUNK, kw), 0) % GLA_DIAG
    zeros = lambda n: jnp.zeros((n, kw), F32)
    q_x, k_x, dec, q_lv, k_lv, f_diag = [], [], [], [], [], []
    for c in range(n_chunk):
        q = q_ref[rows(c), :]
        k = k_ref[rows(c), :]
        bc = b[c]
        b_last = bc[GLA_CHUNK - 1:GLA_CHUNK]
        q_x.append((q * jnp.exp(bc)).astype(BF16))
        k_x.append((k * jnp.exp(b_last - bc)).astype(BF16))
        dec.append(jnp.exp(b_last))
        ql, kl = [], []
        for w in GLA_LEVELS:
            qp, kp = [], []
            for blk in range(GLA_CHUNK // w):
                r = slice(blk * w, (blk + 1) * w)
                if blk % 2:
                    qp.append(q[r] * jnp.exp(bc[r] - bc[blk * w - 1:blk * w]))
                    kp.append(zeros(w))
                else:
                    qp.append(zeros(w))
                    kp.append(k[r] * jnp.exp(bc[(blk + 1) * w - 1:(blk + 1) * w] - bc[r]))
            ql.append(jnp.concatenate(qp, axis=0).astype(BF16))
            kl.append(jnp.concatenate(kp, axis=0).astype(BF16))
        q_lv.append(ql)
        k_lv.append(kl)
        f = [(q * k).astype(BF16)]
        for m in range(1, GLA_DIAG):
            gap = jnp.where(row_in_diag >= m, bc - pltpu.roll(bc, m, 0), NEG_INF)
            f.append((q * pltpu.roll(k, m, 0) * jnp.exp(gap)).astype(BF16))
        f_diag.append(jnp.concatenate(f, axis=0))

    ones = ones_ref[...]
    o_intra, s_inc = [], []
    for c in range(n_chunk):
        r_diag = _dot(f_diag[c], ones)
        v = v_ref[rows(c), :]
        oc, uc = [], []
        for p in range(N_PAIR):
            a = None
            for li in range(len(GLA_LEVELS)):
                k_bd = jnp.concatenate(head_split(k_lv[c][li][:, pk(p)]), axis=0)
                term = level_ref[li] * _dot_nt(q_lv[c][li][:, pk(p)], k_bd)
                a = term if a is None else a + term
            for m in range(GLA_DIAG):
                a = a + band_ref[m] * r_diag[m * GLA_CHUNK:(m + 1) * GLA_CHUNK, pk(p)]
            vp = v[:, pv(p)]
            zv = jnp.zeros((GLA_CHUNK, GLA_V_DIM), BF16)
            v_bd = jnp.concatenate([jnp.concatenate([vp[:, :GLA_V_DIM], zv], axis=1),
                                    jnp.concatenate([zv, vp[:, GLA_V_DIM:]], axis=1)], axis=0)
            oc.append(_dot(a.astype(BF16), v_bd))
            uc.append(state_increment(vp, k_x[c][:, pk(p)]))
        o_intra.append(oc)
        s_inc.append(uc)

    o_inter = [[None] * N_PAIR for _ in range(n_chunk)]
    for p in range(N_PAIR):
        st = st_ref[p]
        for c in range(n_chunk):
            o_inter[c][p] = _dot_nt(q_x[c][:, pk(p)], st.astype(BF16))
            st = st * dec[c][:, pk(p)] + s_inc[c][p]
        st_ref[p] = st

    gn = gn_ref[...]
    for c in range(n_chunk):
        outs = []
        for h in range(GLA_HEADS):
            p, hh = divmod(h, 2)
            sl = slice(hh * GLA_V_DIM, (hh + 1) * GLA_V_DIM)
            o = o_intra[c][p][:, sl] + o_inter[c][p][:, sl]
            yn = o * lax.rsqrt(jnp.mean(o * o, axis=-1, keepdims=True) + RMS_EPS) * gn
            r = r_ref[rows(c), h * GLA_V_DIM:(h + 1) * GLA_V_DIM]
            outs.append(yn * (r / (1.0 + jnp.exp(-r))))
        o_ref[rows(c), :] = jnp.concatenate(outs, axis=1).astype(o_ref.dtype)


def _gla(qg, kg, vg, rg, lg, kg_m, vg_m, lg_m, gn, batch, seq):
    tg = ROW_TILE
    nt = seq // tg
    cur = lambda n: pl.BlockSpec((tg, n), lambda b, t: (b * nt + t, 0))
    consts = _gla_consts()
    return pl.pallas_call(
        _gla_kernel,
        out_shape=jax.ShapeDtypeStruct((batch * seq, GLA_WIDTH), BF16),
        grid=(batch, nt),
        in_specs=[cur(GLA_K_WIDTH), cur(GLA_K_WIDTH), cur(GLA_WIDTH), cur(GLA_WIDTH), cur(GLA_K_WIDTH),
                  _const_spec((N_META, GLA_K_WIDTH)), _const_spec((N_META, GLA_WIDTH)),
                  _const_spec((N_META, GLA_K_WIDTH)), _const_spec((1, GLA_V_DIM))]
                 + [_const_spec(a.shape) for a in consts],
        out_specs=cur(GLA_WIDTH),
        scratch_shapes=[pltpu.VMEM((N_PAIR, PAIR_V, PAIR_K), F32)],
        compiler_params=pltpu.CompilerParams(dimension_semantics=("arbitrary", "arbitrary"),
                                             vmem_limit_bytes=VMEM_LIMIT),
        name="gla",
    )(qg, kg, vg, rg, lg, kg_m, vg_m, lg_m, gn, *consts)


def _ffn_kernel(x_ref, os_ref, og_ref, lng_ref, lnb_ref, wo_ref, l1g_ref, l1b_ref,
                wg_ref, wu_ref, wd_ref, l2g_ref, l2b_ref, out_ref):
    h = _layer_norm(x_ref[...], lng_ref[...], lnb_ref[...])
    mix = _dot(os_ref[...], wo_ref[:SWA_WIDTH, :]) + _dot(og_ref[...], wo_ref[SWA_WIDTH:, :])
    h1 = _layer_norm(ALPHA * h + mix, l1g_ref[...], l1b_ref[...])
    h1b = h1.astype(BF16)
    acc = jnp.zeros(h1.shape, F32)
    for c in range(0, D_FF, FF_CHUNK):
        gate = _dot(h1b, wg_ref[:, c:c + FF_CHUNK])
        up = _dot(h1b, wu_ref[:, c:c + FF_CHUNK])
        a = (gate / (1.0 + jnp.exp(-gate)) * up).astype(BF16)
        acc = acc + _dot(a, wd_ref[c:c + FF_CHUNK, :])
    out_ref[...] = _layer_norm(ALPHA * h1 + acc, l2g_ref[...], l2b_ref[...])


def _ffn(x2d, o_s, o_g, ln_g, ln_b, w_out, l1g, l1b, w_gate, w_up, w_down, l2g, l2b):
    t = x2d.shape[0]
    tm = ROW_TILE
    row = lambda n: pl.BlockSpec((tm, n), lambda i: (i, 0))
    vec = _const_spec((1, D_MODEL))
    return pl.pallas_call(
        _ffn_kernel,
        out_shape=jax.ShapeDtypeStruct((t, D_MODEL), F32),
        grid=(t // tm,),
        in_specs=[row(D_MODEL), row(SWA_WIDTH), row(GLA_WIDTH), vec, vec,
                  _const_spec((D_MODEL, D_MODEL)), vec, vec,
                  _const_spec((D_MODEL, D_FF)), _const_spec((D_MODEL, D_FF)),
                  _const_spec((D_FF, D_MODEL)), vec, vec],
        out_specs=row(D_MODEL),
        compiler_params=pltpu.CompilerParams(dimension_semantics=("arbitrary",),
                                             vmem_limit_bytes=VMEM_LIMIT),
        name="ffn",
    )(x2d, o_s, o_g, ln_g, ln_b, w_out, l1g, l1b, w_gate, w_up, w_down, l2g, l2b)


def kernel(x, meta_tokens, ln_in_g, ln_in_b, w_in, b_in, w_gate_lr2, b_gate_lr2, attn_sinks,
           gla_norm_g, w_out, ln1_g, ln1_b, w_ffn_gate, w_ffn_up, w_ffn_down, ln2_g, ln2_b):
    batch, seq, d = x.shape
    assert d == D_MODEL and seq % ROW_TILE == 0 and w_in.shape[0] == DEPTH
    x2d = x.reshape(batch * seq, d)
    vec = lambda a: a.reshape(1, -1).astype(F32)
    ln_g, ln_b = vec(ln_in_g), vec(ln_in_b)
    pad = LR_PAD - GLA_GATE_RANK
    w_pad = jnp.pad(w_in[0], ((0, 0), (0, pad))).astype(BF16)
    b_pad = jnp.pad(b_in[0], (0, pad)).reshape(1, -1)
    wg2_pad = jnp.pad(w_gate_lr2[0], ((0, pad), (0, 0)))
    bg2 = vec(b_gate_lr2[0])

    in_proj = functools.partial(_in_proj, ln_g=ln_g, ln_b=ln_b, w_pad=w_pad, b_pad=b_pad,
                                wg2_pad=wg2_pad, bg2=bg2)
    qs, ks, vs, qg, kg, vg, rg, lg = in_proj(x2d, tm=ROW_TILE)
    _, ks_m, vs_m, _, kg_m, vg_m, _, lg_m = in_proj(meta_tokens.astype(F32), tm=N_META)

    o_s = _swa(qs, ks, vs, ks_m, vs_m, attn_sinks[0].astype(F32), batch, seq)
    o_g = _gla(qg, kg, vg, rg, lg, kg_m, vg_m, lg_m, vec(gla_norm_g[0]), batch, seq)
    out = _ffn(x2d, o_s, o_g, ln_g, ln_b, w_out[0].astype(BF16), vec(ln1_g[0]), vec(ln1_b[0]),
               w_ffn_gate[0].astype(BF16), w_ffn_up[0].astype(BF16), w_ffn_down[0].astype(BF16),
               vec(ln2_g[0]), vec(ln2_b[0]))
    return out.reshape(batch, seq, d)
```
